```python
import math
import jax, jax.numpy as jnp
from jax import lax
import numpy as np

D_MODEL = 1024
BATCH = 16
SEQ = 2048
DEPTH = 1

PLE_DIM = 256
DA_HEADS = 4
DA_QK = 64
DA_V = 2 * DA_QK
DA_WIDTH = DA_HEADS * DA_V
RW_HEADS = 8
RW_N = 64
RW_WIDTH = RW_HEADS * RW_N
DECAY_LORA = 64
AAA_LORA = 64
GATE_LORA = 160
MIX_WIDTH = DA_WIDTH + RW_WIDTH
N_BUCKETS = 32
MAX_DISTANCE = 128
Q_BLOCK = 128
N_EXPERTS = 16
EC_FACTOR = 2
D_FF = 2048
NORM_EPS = 1e-6
RW_LN_EPS = 64e-5

DA_COLS = 3 * DA_WIDTH
RW_SIZES = [RW_WIDTH, RW_WIDTH, RW_WIDTH, 2 * DECAY_LORA, 2 * AAA_LORA, GATE_LORA]
RW_COLS = sum(RW_SIZES)
IN_COLS = DA_COLS + RW_COLS

kernel_name = "hybrid_diffattn_rwkv7_ecmoe_block"


def rmsnorm(x, g, eps=NORM_EPS):
    xf = x.astype(jnp.float32)
    y = xf * lax.rsqrt(jnp.mean(xf * xf, axis=-1, keepdims=True) + eps)
    return (y * g.astype(jnp.float32)).astype(x.dtype)


def t5_bucket(rel):
    half = N_BUCKETS // 2
    max_exact = half // 2
    ret = jnp.where(rel > 0, half, 0)
    n = jnp.abs(rel)
    nf = jnp.maximum(n, 1).astype(jnp.float32)
    large = max_exact + (jnp.log(nf / max_exact) / math.log(MAX_DISTANCE / max_exact)
                         * (half - max_exact)).astype(jnp.int32)
    large = jnp.minimum(large, half - 1)
    return ret + jnp.where(n < max_exact, n, large)


def diff_attention(q, k, v, lam, rel_bias):
    B, S, H, _, dq = q.shape
    dv = v.shape[-1]
    nb = S // Q_BLOCK
    qb = q.reshape(B, nb, Q_BLOCK, H, 2, dq).transpose(1, 0, 3, 4, 2, 5)
    kt = k.transpose(0, 2, 3, 1, 4)
    vt = v.transpose(0, 2, 1, 3)
    kpos = jnp.arange(S, dtype=jnp.int32)
    scale = DA_QK ** -0.5

    def block(args):
        qblk, start = args
        qpos = start + jnp.arange(Q_BLOCK, dtype=jnp.int32)
        bias = rel_bias[t5_bucket(kpos[None, :] - qpos[:, None])]
        bias = bias.astype(jnp.float32).transpose(2, 0, 1)[None, :, None]
        logits = jnp.einsum('bhcqd,bhckd->bhcqk', qblk, kt,
                            preferred_element_type=jnp.float32) * scale + bias
        probs = jax.nn.softmax(logits, axis=-1)
        attn = probs[:, :, 0] - lam * probs[:, :, 1]
        return jnp.einsum('bhqk,bhkd->bhqd', attn.astype(vt.dtype), vt)

    starts = jnp.arange(nb, dtype=jnp.int32) * Q_BLOCK
    out = lax.map(block, (qb, starts))
    return out.transpose(1, 0, 3, 2, 4).reshape(B, S, H, dv)


def diff_attn_mixer(u, q_g, k_g, lq1, lk1, lq2, lk2, subln_g, rel_bias, layer_idx):
    B, S, _ = u.shape
    q, k, v = jnp.split(u, 3, axis=-1)
    q = rmsnorm(q.reshape(B, S, DA_HEADS, 2, DA_QK), q_g)
    k = rmsnorm(k.reshape(B, S, DA_HEADS, 2, DA_QK), k_g)
    v = v.reshape(B, S, DA_HEADS, DA_V)
    lam_init = 0.8 - 0.6 * math.exp(-0.3 * layer_idx)
    lam = (jnp.exp(jnp.sum(lq1.astype(jnp.float32) * lk1.astype(jnp.float32)))
           - jnp.exp(jnp.sum(lq2.astype(jnp.float32) * lk2.astype(jnp.float32))) + lam_init)
    o = diff_attention(q, k, v, lam, rel_bias)
    o = rmsnorm(o, subln_g) * (1.0 - lam_init)
    return o.reshape(B, S, DA_WIDTH)


def centred_shift_mix(u, mu):
    zero = jnp.zeros_like(u[:, :1])
    prev = jnp.concatenate([zero, u[:, :-1]], axis=1)
    nxt = jnp.concatenate([u[:, 1:], zero], axis=1)
    return u + mu * (0.5 * (prev + nxt) - u)


def rwkv7_bidir_scan(r, w, k, v, kk, a):
    _, B, S, H, N = r.shape
    xs = tuple(jnp.moveaxis(t, 2, 0) for t in (r, w, k, v, kk, a))

    def step(state, inp):
        r_t, w_t, k_t, v_t, kk_t, a_t = inp
        sa = jnp.einsum('dbhvk,dbhk->dbhv', state, -kk_t)
        state = (state * w_t[..., None, :] + sa[..., None] * (kk_t * a_t)[..., None, :]
                 + v_t[..., :, None] * k_t[..., None, :])
        y = jnp.einsum('dbhvk,dbhk->dbhv', state, r_t)
        return state, y

    s0 = jnp.zeros((2, B, H, N, N), jnp.float32)
    _, ys = lax.scan(step, s0, xs)
    return jnp.moveaxis(ys, 0, 2)


def rwkv7_mixer(u, mu, w0, w2, a0, a2, g2, k_k, k_a, r_k, lnx_g, lnx_b):
    in_dtype = u.dtype
    u = centred_shift_mix(u.astype(jnp.float32), mu.astype(jnp.float32))
    B, S, _ = u.shape
    offs = list(np.cumsum(RW_SIZES)[:-1])
    r, k, v, wl, al, gl = jnp.split(u, offs, axis=-1)
    f32 = lambda t: t.astype(jnp.float32)
    wl = wl.reshape(B, S, 2, DECAY_LORA)
    al = al.reshape(B, S, 2, AAA_LORA)
    w_logit = f32(w0)[:, None, None, :] + jnp.einsum('bsdr,drc->dbsc', jnp.tanh(wl), f32(w2))
    decay = jnp.exp(-jnp.exp(-jax.nn.softplus(-w_logit) - 0.5))
    a = jax.nn.sigmoid(f32(a0)[:, None, None, :] + jnp.einsum('bsdr,drc->dbsc', al, f32(a2)))
    g = jax.nn.sigmoid(gl) @ f32(g2)
    kk = (k * f32(k_k)).reshape(B, S, RW_HEADS, RW_N)
    kk = kk / jnp.maximum(jnp.sqrt(jnp.sum(kk * kk, axis=-1, keepdims=True)), 1e-12)
    k_dir = k[None] * (1.0 + (a - 1.0) * f32(k_a))
    heads = lambda t: t.reshape(t.shape[:-1] + (RW_HEADS, RW_N))
    rh, vh = heads(r), heads(v)
    kdh, dh, ah = heads(k_dir), heads(decay), heads(a)
    both = lambda t: jnp.stack([t, t[:, ::-1]])
    orient = lambda t: jnp.stack([t[0], t[1][:, ::-1]])
    ys = rwkv7_bidir_scan(both(rh), orient(dh), orient(kdh), both(vh), both(kk), orient(ah))
    y = ys[0] + ys[1][:, ::-1]
    mean = jnp.mean(y, axis=-1, keepdims=True)
    var = jnp.mean(jnp.square(y - mean), axis=-1, keepdims=True)
    y = (y - mean) * lax.rsqrt(var + RW_LN_EPS)
    y = y * heads(f32(lnx_g)) + heads(f32(lnx_b))
    bonus = jnp.sum(rh * (kdh[0] + kdh[1]) * f32(r_k), axis=-1, keepdims=True) * vh
    out = (y + bonus).reshape(B, S, RW_WIDTH) * g
    return out.astype(in_dtype)


def expert_choice_ffn(x, w_router, w1, w3, w2):
    B, S, _ = x.shape
    cap = EC_FACTOR * S // N_EXPERTS
    aff = jax.nn.softmax((x @ w_router).astype(jnp.float32), axis=-1)
    gate, idx = lax.top_k(jnp.swapaxes(aff, 1, 2), cap)
    xe = jax.vmap(lambda xb, ib: xb[ib])(x, idx)
    hdn = jax.nn.silu(jnp.einsum('becd,edf->becf', xe, w1)) * jnp.einsum('becd,edf->becf', xe, w3)
    ye = jnp.einsum('becf,efd->becd', hdn, w2) * gate[..., None].astype(x.dtype)
    bidx = jnp.arange(B)[:, None, None]
    return jnp.zeros_like(x).at[bidx, idx].add(ye)


def per_layer_embedding(h, p_i, g_in, w_gate, w_pe, g_post):
    gate = jax.nn.sigmoid(rmsnorm(h, g_in) @ w_gate)
    e = rmsnorm(p_i @ w_pe, g_post)
    return h + gate * e


def setup_inputs(seed: int = 0) -> dict:
    key = jax.random.key(seed)
    ks = iter(jax.random.split(key, 48))
    f32 = jnp.float32
    nrm = lambda shape, scale: jax.random.normal(next(ks), shape, f32) * scale
    L = DEPTH
    return {
        "x": nrm((BATCH, SEQ, D_MODEL), 1.0),
        "p": nrm((DEPTH, BATCH, SEQ, PLE_DIM), 1.0),
        "rel_bias": nrm((N_BUCKETS, DA_HEADS), 0.5),
        "g_mix": 1.0 + nrm((L, D_MODEL), 0.05),
        "w_in": nrm((L, D_MODEL, IN_COLS), D_MODEL ** -0.5),
        "w_out": nrm((L, MIX_WIDTH, D_MODEL), MIX_WIDTH ** -0.5),
        "q_norm_g": 1.0 + nrm((L, DA_QK), 0.05),
        "k_norm_g": 1.0 + nrm((L, DA_QK), 0.05),
        "lam_q1": nrm((L, DA_QK), 0.1),
        "lam_k1": nrm((L, DA_QK), 0.1),
        "lam_q2": nrm((L, DA_QK), 0.1),
        "lam_k2": nrm((L, DA_QK), 0.1),
        "subln_g": 1.0 + nrm((L, DA_V), 0.05),
        "rw_mu": jax.random.uniform(next(ks), (L, RW_COLS), f32, 0.2, 0.8),
        "rw_w0": jax.random.uniform(next(ks), (L, 2, RW_WIDTH), f32, -6.0, -1.0),
        "rw_w2": nrm((L, 2, DECAY_LORA, RW_WIDTH), 0.1 * DECAY_LORA ** -0.5),
        "rw_a0": nrm((L, 2, RW_WIDTH), 0.1),
        "rw_a2": nrm((L, 2, AAA_LORA, RW_WIDTH), 0.1 * AAA_LORA ** -0.5),
        "rw_g2": nrm((L, GATE_LORA, RW_WIDTH), GATE_LORA ** -0.5),
        "rw_k_k": 0.85 + nrm((L, RW_WIDTH), 0.05),
        "rw_k_a": 1.0 + nrm((L, RW_WIDTH), 0.05),
        "rw_r_k": nrm((L, RW_HEADS, RW_N), 0.1),
        "rw_lnx_g": 1.0 + nrm((L, RW_WIDTH), 0.05),
        "rw_lnx_b": nrm((L, RW_WIDTH), 0.01),
        "g_ffn": 1.0 + nrm((L, D_MODEL), 0.05),
        "w_router": nrm((L, D_MODEL, N_EXPERTS), D_MODEL ** -0.5),
        "w1": nrm((L, N_EXPERTS, D_MODEL, D_FF), D_MODEL ** -0.5),
        "w3": nrm((L, N_EXPERTS, D_MODEL, D_FF), D_MODEL ** -0.5),
        "w2": nrm((L, N_EXPERTS, D_FF, D_MODEL), D_FF ** -0.5),
        "g_ple": 1.0 + nrm((L, D_MODEL), 0.05),
        "w_ple_gate": nrm((L, D_MODEL, D_MODEL), D_MODEL ** -0.5),
        "w_ple": nrm((L, PLE_DIM, D_MODEL), PLE_DIM ** -0.5),
        "g_ple_post": 1.0 + nrm((L, D_MODEL), 0.05),
    }


def reference(x, p, rel_bias, g_mix, w_in, w_out, q_norm_g, k_norm_g, lam_q1, lam_k1, lam_q2,
              lam_k2, subln_g, rw_mu, rw_w0, rw_w2, rw_a0, rw_a2, rw_g2, rw_k_k, rw_k_a, rw_r_k,
              rw_lnx_g, rw_lnx_b, g_ffn, w_router, w1, w3, w2, g_ple, w_ple_gate, w_ple,
              g_ple_post):
    h = x
    for i in range(DEPTH):
        u = rmsnorm(h, g_mix[i]) @ w_in[i]
        u_da, u_rw = u[..., :DA_COLS], u[..., DA_COLS:]
        o_da = diff_attn_mixer(u_da, q_norm_g[i], k_norm_g[i], lam_q1[i], lam_k1[i],
                               lam_q2[i], lam_k2[i], subln_g[i], rel_bias, i)
        o_rw = rwkv7_mixer(u_rw, rw_mu[i], rw_w0[i], rw_w2[i], rw_a0[i], rw_a2[i], rw_g2[i],
                           rw_k_k[i], rw_k_a[i], rw_r_k[i], rw_lnx_g[i], rw_lnx_b[i])
        h = h + jnp.concatenate([o_da, o_rw], axis=-1) @ w_out[i]
        h = h + expert_choice_ffn(rmsnorm(h, g_ffn[i]), w_router[i], w1[i], w3[i], w2[i])
        h = per_layer_embedding(h, p[i], g_ple[i], w_ple_gate[i], w_ple[i], g_ple_post[i])
    return h
```

```python
import functools
import math

import jax
import jax.numpy as jnp
from jax import lax
from jax.experimental import pallas as pl
from jax.experimental.pallas import tpu as pltpu

F32 = jnp.float32
BF16 = jnp.bfloat16

LANES = 128
SUBLANES = 8
VMEM_LIMIT = 56 * 1024 * 1024

DA_HEADS = 4
DA_QK = 64
DA_V = 2 * DA_QK
DA_WIDTH = DA_HEADS * DA_V
RW_HEADS = 8
RW_N = 64
RW_WIDTH = RW_HEADS * RW_N
DECAY_LORA = 64
AAA_LORA = 64
GATE_LORA = 160
GATE_LORA_PAD = 256
N_BUCKETS = 32
MAX_DISTANCE = 128
N_EXPERTS = 16
EC_FACTOR = 2
NORM_EPS = 1e-6
RW_LN_EPS = 64e-5
DA_COLS = 3 * DA_WIDTH
RW_COLS_PAD = 3 * RW_WIDTH + 2 * DECAY_LORA + 2 * AAA_LORA + GATE_LORA_PAD


def _cparams(*sem):
    return pltpu.CompilerParams(dimension_semantics=sem, vmem_limit_bytes=VMEM_LIMIT)


def _split_bf16(x):
    hi = x.astype(BF16)
    lo = (x - hi.astype(F32)).astype(BF16)
    return hi, lo


def _dot(a, b):
    return jnp.dot(a, b, preferred_element_type=F32)


def _dot_nt(a, b):
    return lax.dot_general(a, b, (((1,), (1,)), ((), ())), preferred_element_type=F32)


def _inproj_kernel(x_ref, g_ref, wda_ref, wrw_ref, uda_ref, urw_ref):
    x = x_ref[...]
    ms = jnp.mean(x * x, axis=-1, keepdims=True)
    xn = (x * lax.rsqrt(ms + NORM_EPS) * g_ref[...]).astype(BF16)
    uda_ref[...] = _dot(xn, wda_ref[...])
    urw_ref[...] = _dot(xn, wrw_ref[...])


def _inproj(x2, g, wda, wrw, tm):
    t, d = x2.shape
    return pl.pallas_call(
        _inproj_kernel,
        grid=(t // tm,),
        in_specs=[
            pl.BlockSpec((tm, d), lambda i: (i, 0)),
            pl.BlockSpec((1, d), lambda i: (0, 0)),
            pl.BlockSpec(wda.shape, lambda i: (0, 0)),
            pl.BlockSpec(wrw.shape, lambda i: (0, 0)),
        ],
        out_specs=[
            pl.BlockSpec((tm, wda.shape[1]), lambda i: (i, 0)),
            pl.BlockSpec((tm, wrw.shape[1]), lambda i: (i, 0)),
        ],
        out_shape=[
            jax.ShapeDtypeStruct((t, wda.shape[1]), F32),
            jax.ShapeDtypeStruct((t, wrw.shape[1]), F32),
        ],
        compiler_params=_cparams("parallel"),
        name="inproj",
    )(x2, g, wda, wrw)


def _t5_bucket(rel):
    half = N_BUCKETS // 2
    max_exact = half // 2
    ret = jnp.where(rel > 0, half, 0)
    n = jnp.abs(rel)
    nf = jnp.maximum(n, 1).astype(F32)
    large = max_exact + (jnp.log(nf / max_exact) / math.log(MAX_DISTANCE / max_exact)
                         * (half - max_exact)).astype(jnp.int32)
    large = jnp.minimum(large, half - 1)
    return ret + jnp.where(n < max_exact, n, large)


def _attn_kernel(relb_ref, q_ref, k_ref, v_ref, qg_ref, kg_ref, sg_ref, lam_ref, o_ref, bias_scr,
                 *, lam_init):
    h = pl.program_id(0)
    qb = pl.program_id(1)
    b = pl.program_id(2)
    tq, s = bias_scr.shape

    @pl.when(b == 0)
    def _():
        qpos = qb * tq + lax.broadcasted_iota(jnp.int32, (tq, s), 0)
        kpos = lax.broadcasted_iota(jnp.int32, (tq, s), 1)
        bucket = _t5_bucket(kpos - qpos)
        acc = jnp.zeros((tq, s), F32)
        for j in range(N_BUCKETS):
            acc = jnp.where(bucket == j, relb_ref[j, h], acc)
        bias_scr[...] = acc

    lo = lax.broadcasted_iota(jnp.int32, (1, LANES), 1) < DA_QK

    def subnorm(t, g):
        sq = t * t
        s_lo = jnp.sum(jnp.where(lo, sq, 0.0), axis=-1, keepdims=True)
        s_hi = jnp.sum(jnp.where(lo, 0.0, sq), axis=-1, keepdims=True)
        ms = jnp.where(lo, s_lo, s_hi) * (1.0 / DA_QK)
        return t * lax.rsqrt(ms + NORM_EPS) * g

    lq = lam_ref[...]
    lam = (jnp.exp(jnp.sum(lq[0:1] * lq[1:2], axis=-1, keepdims=True))
           - jnp.exp(jnp.sum(lq[2:3] * lq[3:4], axis=-1, keepdims=True)) + lam_init)

    qn = subnorm(q_ref[0], qg_ref[...]) * (DA_QK ** -0.5)
    kn = subnorm(k_ref[0], kg_ref[...]).astype(BF16)
    q1 = jnp.where(lo, qn, 0.0).astype(BF16)
    q2 = jnp.where(lo, 0.0, qn).astype(BF16)
    bias = bias_scr[...]

    def softmax(qc):
        l = _dot_nt(qc, kn) + bias
        e = jnp.exp(l - jnp.max(l, axis=-1, keepdims=True))
        return e / jnp.sum(e, axis=-1, keepdims=True)

    attn = softmax(q1) - lam * softmax(q2)
    o = _dot(attn.astype(BF16), v_ref[0].astype(BF16))
    o = o * lax.rsqrt(jnp.mean(o * o, axis=-1, keepdims=True) + NORM_EPS) * sg_ref[...]
    o_ref[0] = o * (1.0 - lam_init)


def _diff_attention(uda, rel_bias, qg2, kg2, sg, lamv, tq, lam_init):
    b, s, _ = uda.shape
    nq = s // tq
    return pl.pallas_call(
        functools.partial(_attn_kernel, lam_init=lam_init),
        grid=(DA_HEADS, nq, b),
        in_specs=[
            pl.BlockSpec(memory_space=pltpu.SMEM),
            pl.BlockSpec((1, tq, DA_V), lambda h, q, i: (i, q, h)),
            pl.BlockSpec((1, s, DA_V), lambda h, q, i: (i, 0, DA_HEADS + h)),
            pl.BlockSpec((1, s, DA_V), lambda h, q, i: (i, 0, 2 * DA_HEADS + h)),
            pl.BlockSpec((1, DA_V), lambda h, q, i: (0, 0)),
            pl.BlockSpec((1, DA_V), lambda h, q, i: (0, 0)),
            pl.BlockSpec((1, DA_V), lambda h, q, i: (0, 0)),
            pl.BlockSpec((4, DA_QK), lambda h, q, i: (0, 0)),
        ],
        out_specs=pl.BlockSpec((1, tq, DA_V), lambda h, q, i: (i, q, h)),
        out_shape=jax.ShapeDtypeStruct((b, s, DA_WIDTH), F32),
        scratch_shapes=[pltpu.VMEM((tq, s), F32)],
        compiler_params=_cparams("arbitrary", "arbitrary", "arbitrary"),
        name="diff_attn",
    )(rel_bias, uda, uda, uda, qg2, kg2, sg, lamv)


def _rwprep_kernel(u_ref, up_ref, un_ref, mu_ref, w0_ref, w2_ref, a0_ref, a2_ref, g2_ref,
                   kk_ref, ka_ref, rk_ref, gsum_ref,
                   r_o, v_o, kkn_o, w_o, k_o, b_o, bonus_o, g_o):
    i = pl.program_id(1)
    n_i = pl.num_programs(1)
    u = u_ref[0]
    tm = u.shape[0]
    row = lax.broadcasted_iota(jnp.int32, (tm, 1), 0)
    prev_row = jnp.where(i > 0, up_ref[0, SUBLANES - 1:SUBLANES, :], 0.0)
    next_row = jnp.where(i < n_i - 1, un_ref[0, 0:1, :], 0.0)
    prev = jnp.where(row == 0, prev_row, pltpu.roll(u, 1, 0))
    nxt = jnp.where(row == tm - 1, next_row, pltpu.roll(u, tm - 1, 0))
    xm = u + mu_ref[...] * (0.5 * (prev + nxt) - u)

    c = RW_WIDTH
    r = xm[:, 0:c]
    k = xm[:, c:2 * c]
    v = xm[:, 2 * c:3 * c]
    o = 3 * c
    wl = xm[:, o:o + 2 * DECAY_LORA]
    al = xm[:, o + 2 * DECAY_LORA:o + 2 * DECAY_LORA + 2 * AAA_LORA]
    gl = xm[:, o + 2 * DECAY_LORA + 2 * AAA_LORA:]

    w_logit = w0_ref[...] + _dot(jnp.tanh(wl).astype(BF16), w2_ref[...])
    decay = jnp.exp(-jnp.exp(-jax.nn.softplus(-w_logit) - 0.5))
    a = jax.nn.sigmoid(a0_ref[...] + _dot(al.astype(BF16), a2_ref[...]))
    g = _dot(jax.nn.sigmoid(gl).astype(BF16), g2_ref[...])

    gsum = gsum_ref[...]

    def head_sum(t):
        hi, lo = _split_bf16(t)
        return _dot(hi, gsum) + _dot(lo, gsum)

    kk = k * kk_ref[...]
    kk = kk / jnp.maximum(jnp.sqrt(head_sum(kk * kk)), 1e-12)
    ka = ka_ref[...]
    kd = [k * (1.0 + (a[:, d * c:(d + 1) * c] - 1.0) * ka) for d in range(2)]
    bonus_o[0] = head_sum(r * (kd[0] + kd[1]) * rk_ref[...]) * v
    g_o[0] = g

    def put(ref, lead, t):
        for h in range(RW_HEADS):
            ref[lead + (0, h)] = t[:, h * RW_N:(h + 1) * RW_N]

    put(r_o, (), r)
    put(v_o, (), v)
    put(kkn_o, (), -kk)
    for d in range(2):
        put(w_o, (d,), decay[:, d * c:(d + 1) * c])
        put(k_o, (d,), kd[d])
        put(b_o, (d,), kk * a[:, d * c:(d + 1) * c])


def _rwprep(urw, mu, w0c, w2blk, a0c, a2blk, g2p, k_k, k_a, r_k, gsum, tm):
    b, s, cols = urw.shape
    n8 = s // SUBLANES
    tb = tm // SUBLANES
    full = lambda arr: pl.BlockSpec(arr.shape, lambda bi, i: (0,) * arr.ndim)
    hshape = (b, RW_HEADS, s, RW_N)
    hspec = pl.BlockSpec((1, RW_HEADS, tm, RW_N), lambda bi, i: (bi, 0, i, 0))
    dshape = (2,) + hshape
    dspec = pl.BlockSpec((2, 1, RW_HEADS, tm, RW_N), lambda bi, i: (0, bi, 0, i, 0))
    tspec = pl.BlockSpec((1, tm, RW_WIDTH), lambda bi, i: (bi, i, 0))
    return pl.pallas_call(
        _rwprep_kernel,
        grid=(b, s // tm),
        in_specs=[
            pl.BlockSpec((1, tm, cols), lambda bi, i: (bi, i, 0)),
            pl.BlockSpec((1, SUBLANES, cols), lambda bi, i: (bi, jnp.maximum(i * tb - 1, 0), 0)),
            pl.BlockSpec((1, SUBLANES, cols), lambda bi, i: (bi, jnp.minimum((i + 1) * tb, n8 - 1), 0)),
            full(mu), full(w0c), full(w2blk), full(a0c), full(a2blk), full(g2p),
            full(k_k), full(k_a), full(r_k), full(gsum),
        ],
        out_specs=[hspec, hspec, hspec, dspec, dspec, dspec, tspec, tspec],
        out_shape=[jax.ShapeDtypeStruct(hshape, F32)] * 3 + [jax.ShapeDtypeStruct(dshape, F32)] * 3
        + [jax.ShapeDtypeStruct((b, s, RW_WIDTH), F32)] * 2,
        compiler_params=_cparams("parallel", "parallel"),
        name="rwkv_prep",
    )(urw, urw, urw, mu, w0c, w2blk, a0c, a2blk, g2p, k_k, k_a, r_k, gsum)


def _scan_kernel(r_ref, v_ref, kkn_ref, w_ref, k_ref, b_ref, y_ref,
                 state, r_t, v_t, kkn_t, w_t, k_t, b_t, y_t):
    d = pl.program_id(0)
    tt, n, nc = r_t.shape

    @pl.when(pl.program_id(1) == 0)
    def _():
        state[...] = jnp.zeros_like(state)

    def load_t(dst, src):
        dst[...] = src.T.reshape(tt, n, nc)

    load_t(r_t, r_ref[...])
    load_t(v_t, v_ref[...])
    load_t(kkn_t, kkn_ref[...])
    load_t(w_t, w_ref[0])
    load_t(k_t, k_ref[0])
    load_t(b_t, b_ref[0])

    def step(s, carry):
        t = jnp.where(d == 0, s, tt - 1 - s)
        sa = jnp.zeros((n, nc), F32)
        for kc in range(n):
            sa = sa + state[kc * n:(kc + 1) * n, :] * kkn_t[t, kc:kc + 1, :]
        vv = v_t[t]
        y = jnp.zeros((n, nc), F32)
        for kc in range(n):
            sk = (state[kc * n:(kc + 1) * n, :] * w_t[t, kc:kc + 1, :]
                  + sa * b_t[t, kc:kc + 1, :] + vv * k_t[t, kc:kc + 1, :])
            state[kc * n:(kc + 1) * n, :] = sk
            y = y + sk * r_t[t, kc:kc + 1, :]
        y_t[t] = y
        return carry

    lax.fori_loop(0, tt, step, 0)
    y_ref[0] = y_t[...].reshape(tt * n, nc).T


def _rwkv_scan(r2, v2, kkn2, w3, k3, b3, tt):
    chains, sn = r2.shape
    nch = sn // (tt * RW_N)
    blk = tt * RW_N
    cidx = lambda d, i: i + d * (nch - 1 - 2 * i)
    sspec = pl.BlockSpec((chains, blk), lambda d, i: (0, cidx(d, i)))
    dspec = pl.BlockSpec((1, chains, blk), lambda d, i: (d, 0, cidx(d, i)))
    tscr = pltpu.VMEM((tt, RW_N, chains), F32)
    return pl.pallas_call(
        _scan_kernel,
        grid=(2, nch),
        in_specs=[sspec, sspec, sspec, dspec, dspec, dspec],
        out_specs=dspec,
        out_shape=jax.ShapeDtypeStruct((2, chains, sn), F32),
        scratch_shapes=[pltpu.VMEM((RW_N * RW_N, chains), F32)] + [tscr] * 7,
        compiler_params=_cparams("arbitrary", "arbitrary"),
        name="rwkv_scan",
    )(r2, v2, kkn2, w3, k3, b3)


def _post_kernel(y_ref, bonus_ref, g_ref, oda_ref, x_ref, wo_ref, lng_ref, lnb_ref, gf_ref, wr_ref,
                 h_ref, xn_ref, aff_ref):
    outs = []
    for h in range(RW_HEADS):
        y = y_ref[0, 0, h] + y_ref[1, 0, h]
        mean = jnp.mean(y, axis=-1, keepdims=True)
        var = jnp.mean(jnp.square(y - mean), axis=-1, keepdims=True)
        y = (y - mean) * lax.rsqrt(var + RW_LN_EPS)
        outs.append(y * lng_ref[:, h * RW_N:(h + 1) * RW_N] + lnb_ref[:, h * RW_N:(h + 1) * RW_N])
    orw = (jnp.concatenate(outs, axis=-1) + bonus_ref[0]) * g_ref[0]
    mix = (_dot(oda_ref[0].astype(BF16), wo_ref[0:DA_WIDTH, :])
           + _dot(orw.astype(BF16), wo_ref[DA_WIDTH:, :]))
    hh = x_ref[0] + mix
    h_ref[0] = hh
    xn = hh * lax.rsqrt(jnp.mean(hh * hh, axis=-1, keepdims=True) + NORM_EPS) * gf_ref[...]
    xn_ref[0] = xn.astype(BF16)
    xh, xl = _split_bf16(xn)
    wh, wl = _split_bf16(wr_ref[...])
    logits = _dot_nt(wh, xh) + _dot_nt(wh, xl) + _dot_nt(wl, xh)
    e = jnp.exp(logits - jnp.max(logits, axis=0, keepdims=True))
    aff_ref[0] = e / jnp.sum(e, axis=0, keepdims=True)


def _post(y5, bonus, g, oda, x, wo, lng, lnb, gf, wrt, tm):
    b, s, d = x.shape
    full = lambda arr: pl.BlockSpec(arr.shape, lambda bi, i: (0,) * arr.ndim)
    tspec = lambda w: pl.BlockSpec((1, tm, w), lambda bi, i: (bi, i, 0))
    return pl.pallas_call(
        _post_kernel,
        grid=(b, s // tm),
        in_specs=[
            pl.BlockSpec((2, 1, RW_HEADS, tm, RW_N), lambda bi, i: (0, bi, 0, i, 0)),
            tspec(RW_WIDTH), tspec(RW_WIDTH), tspec(DA_WIDTH), tspec(d),
            full(wo), full(lng), full(lnb), full(gf), full(wrt),
        ],
        out_specs=[tspec(d), tspec(d), pl.BlockSpec((1, N_EXPERTS, tm), lambda bi, i: (bi, 0, i))],
        out_shape=[jax.ShapeDtypeStruct((b, s, d), F32), jax.ShapeDtypeStruct((b, s, d), BF16),
                   jax.ShapeDtypeStruct((b, N_EXPERTS, s), F32)],
        compiler_params=_cparams("parallel", "parallel"),
        name="post_outproj_router",
    )(y5, bonus, g, oda, x, wo, lng, lnb, gf, wrt)


def _select_kernel(aff_ref, tri_ref, ones_ref, low_ref, pos_ref, *, cap):
    a = aff_ref[0]
    rows = a.shape[0]
    nb = rows // N_EXPERTS
    bits = pltpu.bitcast(a, jnp.int32).reshape(N_EXPERTS, nb, LANES)

    def count(m):
        return jnp.sum(jnp.sum(m.astype(jnp.int32), axis=2, keepdims=True), axis=1, keepdims=True)

    def body(i, thr):
        cand = thr | jnp.left_shift(jnp.int32(1), 30 - i)
        return jnp.where(count(bits >= cand) >= cap, cand, thr)

    thr = lax.fori_loop(0, 31, body, jnp.zeros((N_EXPERTS, 1, 1), jnp.int32))
    gt = bits > thr
    eq = bits == thr
    need = (cap - count(gt)).astype(F32)

    def excl_prefix(m):
        m2 = m.reshape(rows, LANES).astype(BF16)
        inc = _dot(m2, tri_ref[...])
        tot = _dot(m2, ones_ref[...])
        off = _dot(low_ref[...], tot.astype(BF16))
        return (inc + off).reshape(N_EXPERTS, nb, LANES) - m

    eqf = eq.astype(F32)
    tie_ok = jnp.where(excl_prefix(eqf) < need, eqf, 0.0)
    sel = jnp.where(gt, 1.0, tie_ok)
    pos = jnp.where(sel > 0.0, excl_prefix(sel), -1.0)
    pos_ref[0] = pos.reshape(rows, LANES)


def _select(aff2, tri, ones, low, cap):
    b, rows, _ = aff2.shape
    full = lambda arr: pl.BlockSpec(arr.shape, lambda i: (0,) * arr.ndim)
    return pl.pallas_call(
        functools.partial(_select_kernel, cap=cap),
        grid=(b,),
        in_specs=[pl.BlockSpec((1, rows, LANES), lambda i: (i, 0, 0)), full(tri), full(ones), full(low)],
        out_specs=pl.BlockSpec((1, rows, LANES), lambda i: (i, 0, 0)),
        out_shape=jax.ShapeDtypeStruct((b, rows, LANES), F32),
        compiler_params=_cparams("parallel"),
        name="ec_select",
    )(aff2, tri, ones, low)


def _gather_kernel(pos_ref, xn_ref, xe_ref):
    cap = xe_ref.shape[2]
    s = xn_ref.shape[1]
    slot = lax.broadcasted_iota(jnp.int32, (cap, s), 0).astype(F32)
    p = jnp.where(slot == pos_ref[0, 0], 1.0, 0.0).astype(BF16)
    xe_ref[0, 0] = _dot(p, xn_ref[0]).astype(BF16)


def _gather(pos4, xn, cap):
    b, s, d = xn.shape
    return pl.pallas_call(
        _gather_kernel,
        grid=(b, N_EXPERTS),
        in_specs=[pl.BlockSpec((1, 1, 1, s), lambda bi, e: (bi, e, 0, 0)),
                  pl.BlockSpec((1, s, d), lambda bi, e: (bi, 0, 0))],
        out_specs=pl.BlockSpec((1, 1, cap, d), lambda bi, e: (bi, e, 0, 0)),
        out_shape=jax.ShapeDtypeStruct((b, N_EXPERTS, cap, d), BF16),
        compiler_params=_cparams("parallel", "arbitrary"),
        name="ec_gather",
    )(pos4, xn)


def _ffn_kernel(xe_ref, pos_ref, aff_ref, w1_ref, w3_ref, w2_ref, ye_ref, *, ft):
    mb, _, cap, d = xe_ref.shape
    f = w1_ref.shape[2]
    s = pos_ref.shape[3]
    x = xe_ref[...].reshape(mb * cap, d)
    acc = jnp.zeros((mb * cap, d), F32)
    for j in range(f // ft):
        h1 = _dot(x, w1_ref[0, :, j * ft:(j + 1) * ft])
        h3 = _dot(x, w3_ref[0, :, j * ft:(j + 1) * ft])
        hd = (jax.nn.silu(h1) * h3).astype(BF16)
        acc = acc + _dot(hd, w2_ref[0, j * ft:(j + 1) * ft, :])
    slot = lax.broadcasted_iota(jnp.int32, (cap, s), 0).astype(F32)
    for m in range(mb):
        gate = jnp.sum(jnp.where(slot == pos_ref[m, 0], aff_ref[m, 0], 0.0), axis=-1, keepdims=True)
        ye_ref[m, 0] = (acc[m * cap:(m + 1) * cap, :] * gate).astype(BF16)


def _ffn(xe, pos4, aff4, w1, w3, w2, mb, ft):
    b, e, cap, d = xe.shape
    f = w1.shape[2]
    s = pos4.shape[3]
    return pl.pallas_call(
        functools.partial(_ffn_kernel, ft=ft),
        grid=(e, b // mb),
        in_specs=[
            pl.BlockSpec((mb, 1, cap, d), lambda ei, bi: (bi, ei, 0, 0)),
            pl.BlockSpec((mb, 1, 1, s), lambda ei, bi: (bi, ei, 0, 0)),
            pl.BlockSpec((mb, 1, 1, s), lambda ei, bi: (bi, ei, 0, 0)),
            pl.BlockSpec((1, d, f), lambda ei, bi: (ei, 0, 0)),
            pl.BlockSpec((1, d, f), lambda ei, bi: (ei, 0, 0)),
            pl.BlockSpec((1, f, d), lambda ei, bi: (ei, 0, 0)),
        ],
        out_specs=pl.BlockSpec((mb, 1, cap, d), lambda ei, bi: (bi, ei, 0, 0)),
        out_shape=jax.ShapeDtypeStruct((b, e, cap, d), BF16),
        compiler_params=_cparams("arbitrary", "arbitrary"),
        name="ec_ffn",
    )(xe, pos4, aff4, w1, w3, w2)


def _combine_kernel(h_ref, post_ref, ye_ref, p_ref, gp_ref, wg_ref, wp_ref, gq_ref, o_ref):
    tm = h_ref.shape[1]
    ec, d = ye_ref.shape[1], ye_ref.shape[2]
    cap = ec // N_EXPERTS
    slot = lax.broadcasted_iota(jnp.int32, (tm, cap), 1).astype(F32)
    post = post_ref[0]
    pt = jnp.concatenate(
        [jnp.where(slot == post[:, e:e + 1], 1.0, 0.0).astype(BF16) for e in range(N_EXPERTS)], axis=-1)
    h2 = h_ref[0] + _dot(pt, ye_ref[0])
    hn = h2 * lax.rsqrt(jnp.mean(h2 * h2, axis=-1, keepdims=True) + NORM_EPS) * gp_ref[...]
    gate = jax.nn.sigmoid(_dot(hn.astype(BF16), wg_ref[...]))
    pe = _dot(p_ref[0].astype(BF16), wp_ref[...])
    pe = pe * lax.rsqrt(jnp.mean(pe * pe, axis=-1, keepdims=True) + NORM_EPS) * gq_ref[...]
    o_ref[0] = h2 + gate * pe


def _combine(h1, post, ye3, p, gp, wg, wp, gq, tm):
    b, s, d = h1.shape
    full = lambda arr: pl.BlockSpec(arr.shape, lambda bi, i: (0,) * arr.ndim)
    tspec = lambda w: pl.BlockSpec((1, tm, w), lambda bi, i: (bi, i, 0))
    return pl.pallas_call(
        _combine_kernel,
        grid=(b, s // tm),
        in_specs=[tspec(d), tspec(N_EXPERTS),
                  pl.BlockSpec((1,) + ye3.shape[1:], lambda bi, i: (bi, 0, 0)),
                  tspec(p.shape[2]), full(gp), full(wg), full(wp), full(gq)],
        out_specs=tspec(d),
        out_shape=jax.ShapeDtypeStruct((b, s, d), F32),
        compiler_params=_cparams("parallel", "arbitrary"),
        name="ec_combine_ple",
    )(h1, post, ye3, p, gp, wg, wp, gq)


def _block_diag2(w):
    _, r, c = w.shape
    z = jnp.zeros((r, c), w.dtype)
    return jnp.concatenate([jnp.concatenate([w[0], z], axis=1), jnp.concatenate([z, w[1]], axis=1)], axis=0)


def _layer(h, p_i, rel_bias, layer_idx, g_mix, w_in, w_out, q_norm_g, k_norm_g, lam_q1, lam_k1, lam_q2,
           lam_k2, subln_g, rw_mu, rw_w0, rw_w2, rw_a0, rw_a2, rw_g2, rw_k_k, rw_k_a, rw_r_k,
           rw_lnx_g, rw_lnx_b, g_ffn, w_router, w1, w3, w2, g_ple, w_ple_gate, w_ple, g_ple_post):
    b, s, d = h.shape
    t = b * s
    row = lambda v: v.reshape(1, -1).astype(F32)
    tm = min(512, s)

    wda = w_in[:, :DA_COLS].astype(BF16)
    wrw = jnp.pad(w_in[:, DA_COLS:], ((0, 0), (0, GATE_LORA_PAD - GATE_LORA))).astype(BF16)
    uda, urw = _inproj(h.reshape(t, d), row(g_mix), wda, wrw, tm)
    uda = uda.reshape(b, s, DA_COLS)
    urw = urw.reshape(b, s, RW_COLS_PAD)

    lam_init = 0.8 - 0.6 * math.exp(-0.3 * layer_idx)
    lamv = jnp.stack([lam_q1, lam_k1, lam_q2, lam_k2]).astype(F32)
    oda = _diff_attention(uda, rel_bias.astype(F32), row(jnp.tile(q_norm_g, 2)), row(jnp.tile(k_norm_g, 2)),
                          row(subln_g), lamv, min(256, s), lam_init)

    head_of = jnp.arange(RW_WIDTH) // RW_N
    gsum = (head_of[:, None] == head_of[None, :]).astype(BF16)
    mu = jnp.pad(rw_mu, (0, GATE_LORA_PAD - GATE_LORA))
    g2p = jnp.pad(rw_g2, ((0, GATE_LORA_PAD - GATE_LORA), (0, 0))).astype(BF16)
    r4, v4, kkn4, w5, k5, b5, bonus, g = _rwprep(
        urw, row(mu), row(rw_w0), _block_diag2(rw_w2).astype(BF16), row(rw_a0),
        _block_diag2(rw_a2).astype(BF16), g2p, row(rw_k_k), row(rw_k_a), row(rw_r_k), gsum, min(256, s))
    chains = b * RW_HEADS
    tt = min(32, s)
    y = _rwkv_scan(r4.reshape(chains, s * RW_N), v4.reshape(chains, s * RW_N), kkn4.reshape(chains, s * RW_N),
                   w5.reshape(2, chains, s * RW_N), k5.reshape(2, chains, s * RW_N),
                   b5.reshape(2, chains, s * RW_N), tt)
    y5 = y.reshape(2, b, RW_HEADS, s, RW_N)

    h1, xn2, aff = _post(y5, bonus, g, oda, h, w_out.astype(BF16), row(rw_lnx_g), row(rw_lnx_b),
                         row(g_ffn), w_router.T.astype(F32), tm)

    cap = EC_FACTOR * s // N_EXPERTS
    nb = s // LANES
    rows = N_EXPERTS * nb
    li = jnp.arange(LANES)
    tri = (li[:, None] <= li[None, :]).astype(BF16)
    ones = jnp.ones((LANES, LANES), BF16)
    ri = jnp.arange(rows)
    low = ((ri[None, :] < ri[:, None]) & (ri[None, :] // nb == ri[:, None] // nb)).astype(BF16)
    pos = _select(aff.reshape(b, rows, LANES), tri, ones, low, cap)
    pos4 = pos.reshape(b, N_EXPERTS, 1, s)
    aff4 = aff.reshape(b, N_EXPERTS, 1, s)
    post = jnp.transpose(pos.reshape(b, N_EXPERTS, s), (0, 2, 1))

    xe = _gather(pos4, xn2, cap)
    ye = _ffn(xe, pos4, aff4, w1.astype(BF16), w3.astype(BF16), w2.astype(BF16), min(2, b), 512)
    return _combine(h1, post, ye.reshape(b, N_EXPERTS * cap, d), p_i, row(g_ple), w_ple_gate.astype(BF16),
                    w_ple.astype(BF16), row(g_ple_post), tm)


def kernel(x, p, rel_bias, g_mix, w_in, w_out, q_norm_g, k_norm_g, lam_q1, lam_k1, lam_q2, lam_k2, subln_g, rw_mu, rw_w0, rw_w2, rw_a0, rw_a2, rw_g2, rw_k_k, rw_k_a, rw_r_k, rw_lnx_g, rw_lnx_b, g_ffn, w_router, w1, w3, w2, g_ple, w_ple_gate, w_ple, g_ple_post):
    h = x
    per_layer = (g_mix, w_in, w_out, q_norm_g, k_norm_g, lam_q1, lam_k1, lam_q2, lam_k2, subln_g, rw_mu,
                 rw_w0, rw_w2, rw_a0, rw_a2, rw_g2, rw_k_k, rw_k_a, rw_r_k, rw_lnx_g, rw_lnx_b, g_ffn,
                 w_router, w1, w3, w2, g_ple, w_ple_gate, w_ple, g_ple_post)
    for i in range(g_mix.shape[0]):
        h = _layer(h, p[i], rel_bias, i, *[w[i] for w in per_layer])
    return h
```
